```python
import math
import jax, jax.numpy as jnp
from jax import lax
import numpy as np

D_MODEL = 1024
BATCH = 8
SEQ = 2048
DEPTH = 1
DEC_BATCH = 128
DEC_SEQ = 8
PAST_LEN = 16384
PAGE_SIZE = 128

W_LRU = D_MODEL
LRU_BLOCKS = 16
LRU_BLOCK = W_LRU // LRU_BLOCKS
CONV_W = 4
LRU_C = 8.0
GLA_HEADS = 4
GLA_DK = D_MODEL // 2
GLA_DV = D_MODEL
HEAD_K = GLA_DK // GLA_HEADS
HEAD_V = GLA_DV // GLA_HEADS
GK_RANK = 16
GK_TAU = 16.0
GLA_CHUNK = 64
D_MIX = W_LRU + GLA_DV
SPLITS = [W_LRU, W_LRU, GLA_DK, GLA_DK, GLA_DV, GLA_DV, GK_RANK]
D_IN = sum(SPLITS)
EPS = 1e-6

kernel_name = "hymba_rglru_gla_decode_step"


def rmsnorm(x, g):
    xf = x.astype(jnp.float32)
    ms = jnp.mean(xf * xf, axis=-1, keepdims=True)
    return (xf * lax.rsqrt(ms + EPS) * g.astype(jnp.float32)).astype(x.dtype)


def causal_conv(x, buf, w, b):
    L = x.shape[1]
    xp = jnp.concatenate([buf.astype(x.dtype), x], axis=1)
    out = b + sum(xp[:, j:j + L] * w[j] for j in range(CONV_W))
    return out, xp[:, -(CONV_W - 1):]


def rg_lru(x, h0, w_rg, b_rg, w_ig, b_ig, lam):
    Bn, L, _ = x.shape
    xf = x.astype(jnp.float32)
    xb = xf.reshape(Bn, L, LRU_BLOCKS, LRU_BLOCK)
    r = jax.nn.sigmoid(jnp.einsum('blnd,nde->blne', xb, w_rg.astype(jnp.float32)).reshape(Bn, L, W_LRU) + b_rg)
    i = jax.nn.sigmoid(jnp.einsum('blnd,nde->blne', xb, w_ig.astype(jnp.float32)).reshape(Bn, L, W_LRU) + b_ig)
    log_a = LRU_C * r * jax.nn.log_sigmoid(lam.astype(jnp.float32))
    a = jnp.exp(log_a)
    u = jnp.sqrt(-jnp.expm1(2.0 * log_a)) * (i * xf)

    def combine(e1, e2):
        a1, b1 = e1
        a2, b2 = e2
        return a1 * a2, a2 * b1 + b2

    a_cum, h_zero = lax.associative_scan(combine, (a, u), axis=1)
    h = h_zero + a_cum * h0.astype(jnp.float32)[:, None]
    return h.astype(x.dtype), h[:, -1]


def gla(q, k, v, log_alpha, S0):
    Bn, L = q.shape[:2]
    C = math.gcd(L, GLA_CHUNK)
    N = L // C

    def to_chunks(t):
        return t.astype(jnp.float32).reshape(Bn, N, C, GLA_HEADS, t.shape[-1]).transpose(0, 3, 1, 2, 4)

    q, k, v, g = to_chunks(q), to_chunks(k), to_chunks(v), to_chunks(log_alpha)
    b = jnp.cumsum(g, axis=3)
    b_last = b[:, :, :, -1:]
    q_t = q * jnp.exp(b)
    k_t = k * jnp.exp(-b)
    k_end = k * jnp.exp(b_last - b)
    mask = jnp.tril(jnp.ones((C, C), dtype=bool))
    att = jnp.where(mask, jnp.einsum('bhncd,bhnsd->bhncs', q_t, k_t), 0.0)
    o_intra = jnp.einsum('bhncs,bhnsv->bhncv', att, v)
    upd = jnp.einsum('bhncd,bhncv->nbhdv', k_end, v)
    dec = jnp.exp(b_last[:, :, :, 0]).transpose(2, 0, 1, 3)

    def step(S, inp):
        d, u = inp
        return d[..., None] * S + u, S

    S_final, S_starts = lax.scan(step, S0.astype(jnp.float32), (dec, upd))
    o_inter = jnp.einsum('bhncd,nbhdv->bhncv', q_t, S_starts)
    o = (o_intra + o_inter).transpose(0, 2, 3, 1, 4).reshape(Bn, L, GLA_HEADS, HEAD_V)
    return o, S_final


def mixer_layer(x, h0, conv0, S0, g_pre, w_in, conv_w, conv_b, w_rg, b_rg, w_ig, b_ig,
                lru_lambda, w_gk2, b_gk, g_head, w_out, g_post):
    Bn, L, _ = x.shape
    xn = rmsnorm(x, g_pre)
    proj = xn @ w_in
    idx = [int(s) for s in np.cumsum(SPLITS)[:-1]]
    x_lru, z_lru, q, k, v, z_gla, r_gk = jnp.split(proj, idx, axis=-1)
    xc, conv_new = causal_conv(x_lru, conv0, conv_w, conv_b)
    h, h_new = rg_lru(xc, h0, w_rg, b_rg, w_ig, b_ig, lru_lambda)
    y_lru = h * jax.nn.silu(z_lru)
    log_alpha = jax.nn.log_sigmoid((r_gk @ w_gk2 + b_gk).astype(jnp.float32)) / GK_TAU
    q = q.reshape(Bn, L, GLA_HEADS, HEAD_K) * (HEAD_K ** -0.5)
    k = k.reshape(Bn, L, GLA_HEADS, HEAD_K)
    v = v.reshape(Bn, L, GLA_HEADS, HEAD_V)
    log_alpha = log_alpha.reshape(Bn, L, GLA_HEADS, HEAD_K)
    o, S_new = gla(q, k, v, log_alpha, S0)
    o = rmsnorm(o, g_head).reshape(Bn, L, GLA_DV).astype(x.dtype)
    y_gla = o * jax.nn.silu(z_gla)
    y = jnp.concatenate([y_lru, y_gla], axis=-1) @ w_out
    x_out = x + rmsnorm(y, g_post)
    return x_out, h_new.astype(h0.dtype), conv_new.astype(conv0.dtype), S_new.astype(S0.dtype)


def setup_inputs(seed: int = 0) -> dict:
    key = jax.random.key(seed)
    ks = jax.random.split(key, 20)
    f32 = jnp.float32
    nrm = lambda k, s, sc: jax.random.normal(k, s, f32) * sc
    return {
        "x_prompt": nrm(ks[0], (BATCH, SEQ, D_MODEL), 1.0),
        "x_sample": nrm(ks[1], (DEC_BATCH, DEC_SEQ, D_MODEL), 1.0),
        "state_lru_h": nrm(ks[2], (DEPTH, DEC_BATCH, W_LRU), 0.5),
        "state_lru_conv": nrm(ks[3], (DEPTH, DEC_BATCH, CONV_W - 1, W_LRU), 1.0),
        "state_gla": nrm(ks[4], (DEPTH, DEC_BATCH, GLA_HEADS, HEAD_K, HEAD_V), 0.5),
        "g_pre": 1.0 + nrm(ks[5], (DEPTH, D_MODEL), 0.05),
        "w_in": nrm(ks[6], (DEPTH, D_MODEL, D_IN), D_MODEL ** -0.5),
        "conv_w": nrm(ks[7], (DEPTH, CONV_W, W_LRU), CONV_W ** -0.5),
        "conv_b": nrm(ks[8], (DEPTH, W_LRU), 0.01),
        "w_rg": nrm(ks[9], (DEPTH, LRU_BLOCKS, LRU_BLOCK, LRU_BLOCK), LRU_BLOCK ** -0.5),
        "b_rg": nrm(ks[10], (DEPTH, W_LRU), 0.01),
        "w_ig": nrm(ks[11], (DEPTH, LRU_BLOCKS, LRU_BLOCK, LRU_BLOCK), LRU_BLOCK ** -0.5),
        "b_ig": nrm(ks[12], (DEPTH, W_LRU), 0.01),
        "lru_lambda": jax.random.uniform(ks[13], (DEPTH, W_LRU), f32, 4.3, 9.0),
        "w_gk2": nrm(ks[14], (DEPTH, GK_RANK, GLA_DK), GK_RANK ** -0.5),
        "b_gk": jax.random.uniform(ks[15], (DEPTH, GLA_DK), f32, 1.0, 4.0),
        "g_head": 1.0 + nrm(ks[16], (DEPTH, HEAD_V), 0.05),
        "w_out": nrm(ks[17], (DEPTH, D_MIX, D_MODEL), D_MIX ** -0.5),
        "g_post": 1.0 + nrm(ks[18], (DEPTH, D_MODEL), 0.05),
    }


def reference(x_prompt, x_sample, state_lru_h, state_lru_conv, state_gla, g_pre, w_in, conv_w, conv_b,
              w_rg, b_rg, w_ig, b_ig, lru_lambda, w_gk2, b_gk, g_head, w_out, g_post):
    dt = x_prompt.dtype
    xp, xs = x_prompt, x_sample
    hp_l, cp_l, sp_l, hs_l, cs_l, ss_l = [], [], [], [], [], []
    for l in range(DEPTH):
        params = (g_pre[l], w_in[l], conv_w[l], conv_b[l], w_rg[l], b_rg[l], w_ig[l], b_ig[l],
                  lru_lambda[l], w_gk2[l], b_gk[l], g_head[l], w_out[l], g_post[l])
        h0 = jnp.zeros((BATCH, W_LRU), dt)
        c0 = jnp.zeros((BATCH, CONV_W - 1, W_LRU), dt)
        s0 = jnp.zeros((BATCH, GLA_HEADS, HEAD_K, HEAD_V), dt)
        xp, hp, cp, sp = mixer_layer(xp, h0, c0, s0, *params)
        xs, hs, cs, ss = mixer_layer(xs, state_lru_h[l], state_lru_conv[l], state_gla[l], *params)
        hp_l.append(hp); cp_l.append(cp); sp_l.append(sp)
        hs_l.append(hs); cs_l.append(cs); ss_l.append(ss)
    return (xp, xs, jnp.stack(hp_l), jnp.stack(cp_l), jnp.stack(sp_l),
            jnp.stack(hs_l), jnp.stack(cs_l), jnp.stack(ss_l))
```

```python
import functools

import jax
import jax.numpy as jnp
from jax import lax
from jax.experimental import pallas as pl
from jax.experimental.pallas import tpu as pltpu

D_MODEL = 1024
W_LRU = 1024
LRU_BLOCKS = 16
LRU_BLOCK = W_LRU // LRU_BLOCKS
CONV_W = 4
LRU_C = 8.0
GLA_HEADS = 4
GLA_DK = 512
GLA_DV = 1024
HEAD_K = GLA_DK // GLA_HEADS
HEAD_V = GLA_DV // GLA_HEADS
GK_RANK = 16
GK_TAU = 16.0
GLA_CHUNK = 64
EPS = 1e-6

OFF_XL, OFF_ZL, OFF_Q, OFF_K, OFF_V, OFF_ZG, OFF_RG, D_IN = 0, 1024, 2048, 2560, 3072, 4096, 5120, 5136

LANES = 128
SUBLANES = 8
GATE_GROUP = 256
N_GATE_GROUPS = W_LRU // GATE_GROUP
N_SLAB_W = W_LRU // LANES
N_SLAB_K = GLA_DK // LANES

BF16 = jnp.bfloat16
F32 = jnp.float32


def _dot(a, b):
    return jnp.dot(a, b, preferred_element_type=F32)


def _rms(x, g):
    ms = jnp.mean(x * x, axis=-1, keepdims=True)
    return x * lax.rsqrt(ms + EPS) * g


def _sigmoid(x):
    return 1.0 / (1.0 + jnp.exp(-x))


def _log_sigmoid(x):
    return jnp.minimum(x, 0.0) - jnp.log1p(jnp.exp(-jnp.abs(x)))


def _neg_expm1_2x(x):
    t = jnp.tanh(x)
    return (-2.0 * t) / (1.0 - t)


def _gla_chunk_head(q_t, k_t, k_end, v, dec, s_old, causal):
    att = lax.dot_general(q_t, k_t, (((1,), (1,)), ((), ())), preferred_element_type=F32)
    att = jnp.where(causal, att, 0.0).astype(BF16)
    o = _dot(att, v) + _dot(q_t, s_old.astype(BF16))
    dcol = jnp.transpose(jnp.broadcast_to(dec, (HEAD_K, HEAD_K)))
    dmat = jnp.concatenate([dcol] * (HEAD_V // HEAD_K), axis=1)
    upd = _dot(jnp.transpose(k_end).astype(BF16), v)
    return o, dmat * s_old + upd


def _causal_mask(n):
    row_id = lax.broadcasted_iota(jnp.int32, (n, n), 0)
    col_id = lax.broadcasted_iota(jnp.int32, (n, n), 1)
    return row_id >= col_id


def _prompt_kernel(x_ref, g_pre_ref, w_in_ref, cw_ref, cb_ref, wg_ref, b_rg_ref, b_ig_ref, lam_ref,
                   w_gk2_ref, b_gk_ref, g_head_ref, w_out_ref, g_post_ref,
                   y_ref, h_ref, conv_ref, s_ref,
                   a_s, u_s, g_s, xl_s, xn_s, *, nb, t_blk, pitch):
    i = pl.program_id(0)
    n_chunks = t_blk // GLA_CHUNK
    rows_all = nb * t_blk
    hist = SUBLANES - (CONV_W - 1)

    @pl.when(i == 0)
    def _():
        h_ref[...] = jnp.zeros_like(h_ref)
        s_ref[...] = jnp.zeros_like(s_ref)
        xl_s[...] = jnp.zeros_like(xl_s)

    g_pre = g_pre_ref[...]
    c_lru = LRU_C * _log_sigmoid(lam_ref[...])

    x = x_ref[...].reshape(rows_all, D_MODEL)
    xn = _rms(x, g_pre).astype(BF16)
    xn_s[...] = xn
    xl = _dot(xn, w_in_ref[:, OFF_XL:OFF_XL + W_LRU])
    xc_parts = []
    for b in range(nb):
        base = b * pitch
        xl_s[base:base + SUBLANES, :] = xl_s[base + t_blk:base + t_blk + SUBLANES, :]
        xl_b = xl[b * t_blk:(b + 1) * t_blk]
        xl_s[base + SUBLANES:base + SUBLANES + t_blk, :] = xl_b
        xc_b = cb_ref[...] + cw_ref[CONV_W - 1:CONV_W, :] * xl_b
        for k in range(1, CONV_W):
            xc_b = xc_b + cw_ref[CONV_W - 1 - k:CONV_W - k, :] * xl_s[pl.ds(base + SUBLANES - k, t_blk), :]
        xc_parts.append(xc_b)
    xc = jnp.concatenate(xc_parts, axis=0)
    xcb = xc.astype(BF16)
    r_parts, i_parts = [], []
    for g in range(N_GATE_GROUPS):
        pre = _dot(xcb[:, g * GATE_GROUP:(g + 1) * GATE_GROUP], wg_ref[g])
        r_parts.append(pre[:, :GATE_GROUP])
        i_parts.append(pre[:, GATE_GROUP:])
    r = _sigmoid(jnp.concatenate(r_parts, axis=1) + b_rg_ref[...])
    ig = _sigmoid(jnp.concatenate(i_parts, axis=1) + b_ig_ref[...])
    log_a = c_lru * r
    a = jnp.exp(log_a)
    u = jnp.sqrt(_neg_expm1_2x(log_a)) * (ig * xc)
    rg = _dot(xn, w_in_ref[:, OFF_RG:D_IN]).astype(BF16)
    gl = _log_sigmoid(_dot(rg, w_gk2_ref[...]) + b_gk_ref[...]) * (1.0 / GK_TAU)
    for b in range(nb):
        src = slice(b * t_blk, (b + 1) * t_blk)
        dst = slice(b * pitch, b * pitch + t_blk)
        for j in range(N_SLAB_W):
            a_s[j, dst, :] = a[src, j * LANES:(j + 1) * LANES]
            u_s[j, dst, :] = u[src, j * LANES:(j + 1) * LANES]
        for j in range(N_SLAB_K):
            g_s[j, dst, :] = gl[src, j * LANES:(j + 1) * LANES]

    h0 = tuple(h_ref[:, j * LANES:(j + 1) * LANES] for j in range(N_SLAB_W))

    def chunk_body(c, h):
        def step(t, st):
            h, bc = st
            tt = c * GLA_CHUNK + t
            h_new, bc_new = [], []
            for j in range(N_SLAB_W):
                idx = (j, pl.ds(tt, nb, stride=pitch), slice(None))
                hj = a_s[idx] * h[j] + u_s[idx]
                u_s[idx] = hj
                h_new.append(hj)
            for j in range(N_SLAB_K):
                idx = (j, pl.ds(tt, nb, stride=pitch), slice(None))
                bj = bc[j] + g_s[idx]
                g_s[idx] = bj
                bc_new.append(bj)
            return tuple(h_new), tuple(bc_new)

        bc0 = tuple(jnp.zeros((nb, LANES), F32) for _ in range(N_SLAB_K))
        h, _ = lax.fori_loop(0, GLA_CHUNK, step, (h, bc0))
        return h

    h_fin = lax.fori_loop(0, n_chunks, chunk_body, h0)
    for j in range(N_SLAB_W):
        h_ref[:, j * LANES:(j + 1) * LANES] = h_fin[j]

    xn = xn_s[...]
    h_all = jnp.concatenate(
        [jnp.concatenate([u_s[j, b * pitch:b * pitch + t_blk, :] for j in range(N_SLAB_W)], axis=1)
         for b in range(nb)], axis=0)
    bcum = jnp.concatenate(
        [jnp.concatenate([g_s[j, b * pitch:b * pitch + t_blk, :] for j in range(N_SLAB_K)], axis=1)
         for b in range(nb)], axis=0)
    z = _dot(xn, w_in_ref[:, OFF_ZL:OFF_ZL + W_LRU])
    y_lru = (h_all * (z * _sigmoid(z))).astype(BF16)

    q = _dot(xn, w_in_ref[:, OFF_Q:OFF_Q + GLA_DK]) * (HEAD_K ** -0.5)
    k = _dot(xn, w_in_ref[:, OFF_K:OFF_K + GLA_DK])
    v = _dot(xn, w_in_ref[:, OFF_V:OFF_V + GLA_DV]).astype(BF16)
    zg = _dot(xn, w_in_ref[:, OFF_ZG:OFF_ZG + GLA_DV])

    causal = _causal_mask(GLA_CHUNK)
    g_head = g_head_ref[...]

    o_rows = []
    for b in range(nb):
        for c in range(n_chunks):
            rows = slice(b * t_blk + c * GLA_CHUNK, b * t_blk + (c + 1) * GLA_CHUNK)
            bc = bcum[rows]
            bl = bc[GLA_CHUNK - 1:GLA_CHUNK, :]
            q_t = (q[rows] * jnp.exp(bc)).astype(BF16)
            k_t = (k[rows] * jnp.exp(-bc)).astype(BF16)
            k_end = k[rows] * jnp.exp(bl - bc)
            dec = jnp.exp(bl)
            o_heads = []
            for hd in range(GLA_HEADS):
                ks = slice(hd * HEAD_K, (hd + 1) * HEAD_K)
                vs = slice(hd * HEAD_V, (hd + 1) * HEAD_V)
                o, s_new = _gla_chunk_head(q_t[:, ks], k_t[:, ks], k_end[:, ks], v[rows, vs], dec[:, ks],
                                           s_ref[b, hd], causal)
                s_ref[b, hd] = s_new
                o_heads.append(_rms(o, g_head))
            o_rows.append(jnp.concatenate(o_heads, axis=1))
    o_all = jnp.concatenate(o_rows, axis=0)
    y_gla = (o_all * (zg * _sigmoid(zg))).astype(BF16)
    y = _dot(jnp.concatenate([y_lru, y_gla], axis=1), w_out_ref[...])
    out = x_ref[...].reshape(rows_all, D_MODEL) + _rms(y, g_post_ref[...])
    y_ref[...] = out.reshape(nb, t_blk, D_MODEL)

    @pl.when(i == pl.num_programs(0) - 1)
    def _():
        for b in range(nb):
            conv_ref[b] = xl_s[b * pitch + t_blk + hist:b * pitch + t_blk + SUBLANES, :]


def _spread_rows(slab, by_step, n_seq, t_len, width):
    n_slab = width // LANES
    slab[0:n_slab] = jnp.zeros((n_slab,) + slab.shape[1:], F32)
    for t, val in by_step.items():
        for j in range(n_slab):
            slab[j, pl.ds(t, n_seq, stride=t_len), :] = val[:, j * LANES:(j + 1) * LANES]
    return jnp.concatenate([slab[j] for j in range(n_slab)], axis=1)


def _gather_step(slab, arr, t, n_seq, t_len):
    n_slab = arr.shape[1] // LANES
    for j in range(n_slab):
        slab[j] = arr[:, j * LANES:(j + 1) * LANES]
    return jnp.concatenate([slab[j, pl.ds(t, n_seq, stride=t_len), :] for j in range(n_slab)], axis=1)


def _sample_pre_kernel(x_ref, h0_ref, conv0_ref, g_pre_ref, w_in_ref, cw_ref, cb_ref, wg_ref, b_rg_ref, b_ig_ref,
                       lam_ref, w_gk2_ref, b_gk_ref,
                       qt_ref, kt_ref, ke_ref, v_ref, dec_ref, ylru_ref, szg_ref, hnew_ref, convnew_ref,
                       slab_a, slab_b, *, n_seq, t_len):
    rows = n_seq * t_len
    hist = t_len - (CONV_W - 1)

    def steps(v):
        return v.reshape(n_seq, t_len, v.shape[-1])

    def t_index(width):
        return lax.broadcasted_iota(jnp.int32, (n_seq, t_len, width), 1)

    x = x_ref[...]
    xn = _rms(x, g_pre_ref[...]).astype(BF16)

    xl = _dot(xn, w_in_ref[:, OFF_XL:OFF_XL + W_LRU])
    past = _spread_rows(slab_a, {hist + m: conv0_ref[:, m * W_LRU:(m + 1) * W_LRU] for m in range(CONV_W - 1)},
                        n_seq, t_len, W_LRU)
    xl3, past3 = steps(xl), steps(past)
    t_w = t_index(W_LRU)
    xc3 = cb_ref[...][None] + cw_ref[CONV_W - 1:CONV_W, :][None] * xl3
    for k in range(1, CONV_W):
        src = jnp.where(t_w <= t_len - 1 - k, xl3, past3)
        xc3 = xc3 + cw_ref[CONV_W - 1 - k:CONV_W - k, :][None] * pltpu.roll(src, k, axis=1)
    xc = xc3.reshape(rows, W_LRU)
    xcb = xc.astype(BF16)
    r_parts, i_parts = [], []
    for g in range(N_GATE_GROUPS):
        pre = _dot(xcb[:, g * GATE_GROUP:(g + 1) * GATE_GROUP], wg_ref[g])
        r_parts.append(pre[:, :GATE_GROUP])
        i_parts.append(pre[:, GATE_GROUP:])
    r = _sigmoid(jnp.concatenate(r_parts, axis=1) + b_rg_ref[...])
    ig = _sigmoid(jnp.concatenate(i_parts, axis=1) + b_ig_ref[...])
    log_a = (LRU_C * _log_sigmoid(lam_ref[...])) * r
    a_cum = steps(jnp.exp(log_a))
    h_acc = steps(jnp.sqrt(_neg_expm1_2x(log_a)) * (ig * xc))
    shift = 1
    while shift < t_len:
        m = t_w >= shift
        h_prev = jnp.where(m, pltpu.roll(h_acc, shift, axis=1), 0.0)
        a_prev = jnp.where(m, pltpu.roll(a_cum, shift, axis=1), 1.0)
        h_acc = a_cum * h_prev + h_acc
        a_cum = a_cum * a_prev
        shift *= 2
    h0_all = steps(_spread_rows(slab_b, {t: h0_ref[...] for t in range(t_len)}, n_seq, t_len, W_LRU))
    h = (h_acc + a_cum * h0_all).reshape(rows, W_LRU)
    hnew_ref[...] = _gather_step(slab_a, h, t_len - 1, n_seq, t_len)
    for j in range(N_SLAB_W):
        slab_b[j] = xl[:, j * LANES:(j + 1) * LANES]
    for m in range(CONV_W - 1):
        for j in range(N_SLAB_W):
            convnew_ref[:, m * W_LRU + j * LANES:m * W_LRU + (j + 1) * LANES] = (
                slab_b[j, pl.ds(hist + m, n_seq, stride=t_len), :])
    z = _dot(xn, w_in_ref[:, OFF_ZL:OFF_ZL + W_LRU])
    ylru_ref[...] = (h * (z * _sigmoid(z))).astype(BF16)

    q = _dot(xn, w_in_ref[:, OFF_Q:OFF_Q + GLA_DK]) * (HEAD_K ** -0.5)
    k = _dot(xn, w_in_ref[:, OFF_K:OFF_K + GLA_DK])
    v_ref[...] = _dot(xn, w_in_ref[:, OFF_V:OFF_V + GLA_DV])
    zg = _dot(xn, w_in_ref[:, OFF_ZG:OFF_ZG + GLA_DV])
    szg_ref[...] = zg * _sigmoid(zg)
    rg = _dot(xn, w_in_ref[:, OFF_RG:D_IN]).astype(BF16)
    gl = _log_sigmoid(_dot(rg, w_gk2_ref[...]) + b_gk_ref[...]) * (1.0 / GK_TAU)
    t_k = t_index(GLA_DK)
    bc = steps(gl)
    tot = bc
    shift = 1
    while shift < t_len:
        bc = bc + jnp.where(t_k >= shift, pltpu.roll(bc, shift, axis=1), 0.0)
        tot = tot + pltpu.roll(tot, shift, axis=1)
        shift *= 2
    bc = bc.reshape(rows, GLA_DK)
    tot = tot.reshape(rows, GLA_DK)
    qt_ref[...] = q * jnp.exp(bc)
    kt_ref[...] = k * jnp.exp(-bc)
    ke_ref[...] = k * jnp.exp(tot - bc)
    dec_ref[...] = _gather_step(slab_a, jnp.exp(tot), 0, n_seq, t_len)


def _sample_state_kernel(qt_ref, kt_ref, ke_ref, v_ref, dec_ref, s0_ref, o_ref, s_ref, *, n_seq, t_len):
    causal = _causal_mask(t_len)
    for b in range(n_seq):
        rows = slice(b * t_len, (b + 1) * t_len)
        for hd in range(GLA_HEADS):
            ks = slice(hd * HEAD_K, (hd + 1) * HEAD_K)
            vs = slice(hd * HEAD_V, (hd + 1) * HEAD_V)
            o, s_new = _gla_chunk_head(qt_ref[rows, ks].astype(BF16), kt_ref[rows, ks].astype(BF16),
                                       ke_ref[rows, ks], v_ref[rows, vs].astype(BF16), dec_ref[b:b + 1, ks],
                                       s0_ref[b, hd], causal)
            s_ref[b, hd] = s_new
            o_ref[rows, vs] = o


def _sample_post_kernel(x_ref, o_ref, szg_ref, ylru_ref, g_head_ref, w_out_ref, g_post_ref, y_ref):
    g_head = g_head_ref[...]
    o_n = jnp.concatenate(
        [_rms(o_ref[:, hd * HEAD_V:(hd + 1) * HEAD_V], g_head) for hd in range(GLA_HEADS)], axis=1)
    y_gla = (o_n * szg_ref[...]).astype(BF16)
    y = _dot(jnp.concatenate([ylru_ref[...], y_gla], axis=1), w_out_ref[...])
    y_ref[...] = x_ref[...] + _rms(y, g_post_ref[...])


def _sample_call(x, h0, conv0, s0, params, seq_blk, state_blk):
    (g_pre, w_in, cw, cb, wg, b_rg, b_ig, lam, w_gk2, b_gk, g_head, w_out, g_post) = params
    n_seq, t_len, _ = x.shape
    assert t_len == SUBLANES and t_len >= CONV_W - 1
    rows = n_seq * t_len
    x2 = x.reshape(rows, D_MODEL)
    conv0 = conv0.reshape(n_seq, (CONV_W - 1) * W_LRU)
    rb = seq_blk * t_len

    def row_spec(width, blk_rows):
        return pl.BlockSpec((blk_rows, width), lambda i: (i, 0))

    cparams = pltpu.CompilerParams(dimension_semantics=("arbitrary",), vmem_limit_bytes=56 * 1024 * 1024)
    pre_params = (g_pre, w_in, cw, cb, wg, b_rg, b_ig, lam, w_gk2, b_gk)
    qt, kt, ke, v, dec, ylru, szg, h_new, conv_new = pl.pallas_call(
        functools.partial(_sample_pre_kernel, n_seq=seq_blk, t_len=t_len),
        grid=(n_seq // seq_blk,),
        in_specs=[row_spec(D_MODEL, rb), row_spec(W_LRU, seq_blk), row_spec((CONV_W - 1) * W_LRU, seq_blk)]
        + [_vmem_full()] * len(pre_params),
        out_specs=(row_spec(GLA_DK, rb), row_spec(GLA_DK, rb), row_spec(GLA_DK, rb), row_spec(GLA_DV, rb),
                   row_spec(GLA_DK, seq_blk), row_spec(W_LRU, rb), row_spec(GLA_DV, rb),
                   row_spec(W_LRU, seq_blk), row_spec((CONV_W - 1) * W_LRU, seq_blk)),
        out_shape=(jax.ShapeDtypeStruct((rows, GLA_DK), F32), jax.ShapeDtypeStruct((rows, GLA_DK), F32),
                   jax.ShapeDtypeStruct((rows, GLA_DK), F32), jax.ShapeDtypeStruct((rows, GLA_DV), F32),
                   jax.ShapeDtypeStruct((n_seq, GLA_DK), F32), jax.ShapeDtypeStruct((rows, W_LRU), BF16),
                   jax.ShapeDtypeStruct((rows, GLA_DV), F32), jax.ShapeDtypeStruct((n_seq, W_LRU), F32),
                   jax.ShapeDtypeStruct((n_seq, (CONV_W - 1) * W_LRU), F32)),
        scratch_shapes=[pltpu.VMEM((N_SLAB_W, rb, LANES), F32), pltpu.VMEM((N_SLAB_W, rb, LANES), F32)],
        compiler_params=cparams,
        name="sample_pre",
    )(x2, h0, conv0, *pre_params)

    sb = state_blk * t_len
    s_spec = pl.BlockSpec((state_blk, GLA_HEADS, HEAD_K, HEAD_V), lambda i: (i, 0, 0, 0))
    o, s_new = pl.pallas_call(
        functools.partial(_sample_state_kernel, n_seq=state_blk, t_len=t_len),
        grid=(n_seq // state_blk,),
        in_specs=[row_spec(GLA_DK, sb), row_spec(GLA_DK, sb), row_spec(GLA_DK, sb), row_spec(GLA_DV, sb),
                  row_spec(GLA_DK, state_blk), s_spec],
        out_specs=(row_spec(GLA_DV, sb), s_spec),
        out_shape=(jax.ShapeDtypeStruct((rows, GLA_DV), F32),
                   jax.ShapeDtypeStruct((n_seq, GLA_HEADS, HEAD_K, HEAD_V), F32)),
        compiler_params=cparams,
        name="sample_state",
    )(qt, kt, ke, v, dec, s0)

    y = pl.pallas_call(
        _sample_post_kernel,
        grid=(n_seq // seq_blk,),
        in_specs=[row_spec(D_MODEL, rb), row_spec(GLA_DV, rb), row_spec(GLA_DV, rb), row_spec(W_LRU, rb),
                  _vmem_full(), _vmem_full(), _vmem_full()],
        out_specs=row_spec(D_MODEL, rb),
        out_shape=jax.ShapeDtypeStruct((rows, D_MODEL), F32),
        compiler_params=cparams,
        name="sample_post",
    )(x2, o, szg, ylru, g_head, w_out, g_post)
    return (y.reshape(n_seq, t_len, D_MODEL), h_new, conv_new.reshape(n_seq, CONV_W - 1, W_LRU), s_new)


def _gate_weights(w_rg, w_ig):
    per = GATE_GROUP // LRU_BLOCK

    def bd(w):
        w = w.reshape(N_GATE_GROUPS, per, LRU_BLOCK, LRU_BLOCK)
        eye = jnp.eye(per, dtype=w.dtype)
        w = w[:, :, :, None, :] * eye[None, :, None, :, None]
        return w.reshape(N_GATE_GROUPS, GATE_GROUP, GATE_GROUP)

    return jnp.concatenate([bd(w_rg), bd(w_ig)], axis=2).astype(BF16)


def _vmem_full():
    return pl.BlockSpec(memory_space=pltpu.VMEM)


def _prompt_call(x, params, t_blk):
    nb, seq, _ = x.shape
    pitch = t_blk + SUBLANES
    n_t = seq // t_blk
    kern = functools.partial(_prompt_kernel, nb=nb, t_blk=t_blk, pitch=pitch)
    x_spec = pl.BlockSpec((nb, t_blk, D_MODEL), lambda i: (0, i, 0))
    out_shape = (
        jax.ShapeDtypeStruct((nb, seq, D_MODEL), F32),
        jax.ShapeDtypeStruct((nb, W_LRU), F32),
        jax.ShapeDtypeStruct((nb, CONV_W - 1, W_LRU), F32),
        jax.ShapeDtypeStruct((nb, GLA_HEADS, HEAD_K, HEAD_V), F32),
    )
    out_specs = (
        x_spec,
        pl.BlockSpec((nb, W_LRU), lambda i: (0, 0)),
        pl.BlockSpec((nb, CONV_W - 1, W_LRU), lambda i: (0, 0, 0)),
        pl.BlockSpec((nb, GLA_HEADS, HEAD_K, HEAD_V), lambda i: (0, 0, 0, 0)),
    )
    scratch = [
        pltpu.VMEM((N_SLAB_W, nb * pitch, LANES), F32),
        pltpu.VMEM((N_SLAB_W, nb * pitch, LANES), F32),
        pltpu.VMEM((N_SLAB_K, nb * pitch, LANES), F32),
        pltpu.VMEM((nb * pitch, W_LRU), F32),
        pltpu.VMEM((nb * t_blk, D_MODEL), BF16),
    ]
    return pl.pallas_call(
        kern,
        grid=(n_t,),
        in_specs=[x_spec] + [_vmem_full()] * len(params),
        out_specs=out_specs,
        out_shape=out_shape,
        scratch_shapes=scratch,
        compiler_params=pltpu.CompilerParams(
            dimension_semantics=("arbitrary",),
            vmem_limit_bytes=56 * 1024 * 1024,
        ),
        name="prompt_mixer",
    )(x, *params)


def kernel(x_prompt, x_sample, state_lru_h, state_lru_conv, state_gla, g_pre, w_in, conv_w, conv_b, w_rg, b_rg,
           w_ig, b_ig, lru_lambda, w_gk2, b_gk, g_head, w_out, g_post):
    row = lambda p: p[0].reshape(1, -1)
    params = (
        row(g_pre), w_in[0].astype(BF16), conv_w[0], row(conv_b), _gate_weights(w_rg[0], w_ig[0]),
        row(b_rg), row(b_ig), row(lru_lambda), w_gk2[0].astype(BF16), row(b_gk), row(g_head),
        w_out[0].astype(BF16), row(g_post),
    )
    yp, hp, cp, sp = _prompt_call(x_prompt, params, t_blk=64)
    ys, hs, cs, ss = _sample_call(x_sample, state_lru_h[0], state_lru_conv[0], state_gla[0], params,
                                  seq_blk=64, state_blk=8)
    return yp, ys, hp[None], cp[None], sp[None], hs[None], cs[None], ss[None]
```

```python
import functools

import jax
import jax.numpy as jnp
from jax import lax
from jax.experimental import pallas as pl
from jax.experimental.pallas import tpu as pltpu

D_MODEL = 1024
W_LRU = 1024
LRU_BLOCKS = 16
LRU_BLOCK = W_LRU // LRU_BLOCKS
CONV_W = 4
LRU_C = 8.0
GLA_HEADS = 4
GLA_DK = 512
GLA_DV = 1024
HEAD_K = GLA_DK // GLA_HEADS
HEAD_V = GLA_DV // GLA_HEADS
GK_RANK = 16
GK_TAU = 16.0
GLA_CHUNK = 64
EPS = 1e-6

OFF_XL, OFF_ZL, OFF_Q, OFF_K, OFF_V, OFF_ZG, OFF_RG, D_IN = 0, 1024, 2048, 2560, 3072, 4096, 5120, 5136

LANES = 128
SUBLANES = 8
GATE_GROUP = 256
N_GATE_GROUPS = W_LRU // GATE_GROUP
N_SLAB_W = W_LRU // LANES
N_SLAB_K = GLA_DK // LANES

BF16 = jnp.bfloat16
F32 = jnp.float32


def _dot(a, b):
    return jnp.dot(a, b, preferred_element_type=F32)


def _rms(x, g):
    ms = jnp.mean(x * x, axis=-1, keepdims=True)
    return x * lax.rsqrt(ms + EPS) * g


def _sigmoid(x):
    return 1.0 / (1.0 + jnp.exp(-x))


def _log_sigmoid(x):
    return jnp.minimum(x, 0.0) - jnp.log(1.0 + jnp.exp(-jnp.abs(x)))


def _sqrt_nonneg(x):
    return jnp.where(x > 0.0, x * lax.rsqrt(x), 0.0)


def _neg_expm1_2x(x):
    t = jnp.tanh(x)
    return (-2.0 * t) / (1.0 - t)


def _gla_chunk_heads(items, causal):
    atts = [lax.dot_general(q_t, k_t, (((1,), (1,)), ((), ())), preferred_element_type=F32)
            for (q_t, k_t, _, _, _, _) in items]
    atts = [jnp.where(causal, att, 0.0).astype(BF16) for att in atts]
    outs = [_dot(jnp.concatenate([q_t, att], axis=1), jnp.concatenate([s_old.astype(BF16), v], axis=0))
            for att, (q_t, _, _, v, _, s_old) in zip(atts, items)]
    upds = [_dot(jnp.transpose(k_end).astype(BF16), v) for (_, _, k_end, v, _, _) in items]
    new_states = []
    for upd, (_, _, _, _, dec, s_old) in zip(upds, items):
        dcol = jnp.transpose(jnp.broadcast_to(dec, (HEAD_K, HEAD_K)))
        dmat = jnp.concatenate([dcol] * (HEAD_V // HEAD_K), axis=1)
        new_states.append(dmat * s_old + upd)
    return list(zip(outs, new_states))


def _causal_mask(n):
    row_id = lax.broadcasted_iota(jnp.int32, (n, n), 0)
    col_id = lax.broadcasted_iota(jnp.int32, (n, n), 1)
    return row_id >= col_id


def _prompt_kernel(x_ref, g_pre_ref, w_in_ref, cw_ref, cb_ref, wg_ref, b_rg_ref, b_ig_ref, lam_ref,
                   w_gk2_ref, b_gk_ref, g_head_ref, w_out_ref, g_post_ref,
                   y_ref, h_ref, conv_ref, s_ref,
                   a_s, u_s, g_s, xl_s, sz_s, q_s, k_s, v_s, szg_s, ycat_s, *, nb, t_blk, pitch):
    i = pl.program_id(0)
    n_chunks = t_blk // GLA_CHUNK
    rows_all = nb * t_blk
    hist = SUBLANES - (CONV_W - 1)

    @pl.when(i == 0)
    def _():
        h_ref[...] = jnp.zeros_like(h_ref)
        s_ref[...] = jnp.zeros_like(s_ref)
        xl_s[...] = jnp.zeros_like(xl_s)

    g_pre = g_pre_ref[...]
    c_lru = LRU_C * _log_sigmoid(lam_ref[...])

    x = x_ref[...].reshape(rows_all, D_MODEL)
    xn = _rms(x, g_pre).astype(BF16)
    def to_slabs(dst_ref, val, first_slab):
        for b in range(nb):
            for jj in range(val.shape[1] // LANES):
                dst_ref[first_slab + jj, b * pitch:b * pitch + t_blk, :] = (
                    val[b * t_blk:(b + 1) * t_blk, jj * LANES:(jj + 1) * LANES])

    def proj(off, width=GATE_GROUP):
        return _dot(xn, w_in_ref[:, off:off + width])

    def side_z(ct):
        def run():
            cols = slice(ct * GATE_GROUP, (ct + 1) * GATE_GROUP)
            z = proj(OFF_ZL + ct * GATE_GROUP)
            sz_s[:, cols] = z * _sigmoid(z)
        return run

    def side_zg(ct):
        def run():
            cols = slice(ct * GATE_GROUP, (ct + 1) * GATE_GROUP)
            zg = proj(OFF_ZG + ct * GATE_GROUP)
            szg_s[:, cols] = zg * _sigmoid(zg)
        return run

    def side_v(ct):
        def run():
            cols = slice(ct * GATE_GROUP, (ct + 1) * GATE_GROUP)
            v_s[:, cols] = proj(OFF_V + ct * GATE_GROUP).astype(BF16)
        return run

    def side_q(ct):
        def run():
            cols = slice(ct * GATE_GROUP, (ct + 1) * GATE_GROUP)
            q_s[:, cols] = proj(OFF_Q + ct * GATE_GROUP) * (HEAD_K ** -0.5)
        return run

    def side_k(ct):
        def run():
            cols = slice(ct * GATE_GROUP, (ct + 1) * GATE_GROUP)
            k_s[:, cols] = proj(OFF_K + ct * GATE_GROUP)
        return run

    def side_decay():
        rg = proj(OFF_RG, D_IN - OFF_RG).astype(BF16)
        gl = _log_sigmoid(_dot(rg, w_gk2_ref[...]) + b_gk_ref[...]) * (1.0 / GK_TAU)
        to_slabs(g_s, gl, 0)

    side = [side_decay]
    for ct in range(W_LRU // GATE_GROUP):
        side += [side_z(ct), side_v(ct), side_zg(ct)]
    side += [side_q(ct) for ct in range(GLA_DK // GATE_GROUP)] + [side_k(ct) for ct in range(GLA_DK // GATE_GROUP)]
    side.reverse()

    def filler(n=1):
        for _ in range(n):
            if side:
                side.pop()()

    xl_tiles = {0: proj(OFF_XL)}
    for g in range(N_GATE_GROUPS):
        cols = slice(g * GATE_GROUP, (g + 1) * GATE_GROUP)
        if g + 1 < N_GATE_GROUPS:
            xl_tiles[g + 1] = proj(OFF_XL + (g + 1) * GATE_GROUP)
        xl_g = xl_tiles.pop(g)
        xc_parts = []
        for b in range(nb):
            base = b * pitch
            xl_s[base:base + SUBLANES, cols] = xl_s[base + t_blk:base + t_blk + SUBLANES, cols]
            xl_b = xl_g[b * t_blk:(b + 1) * t_blk]
            xl_s[base + SUBLANES:base + SUBLANES + t_blk, cols] = xl_b
            xc_b = cb_ref[:, cols] + cw_ref[CONV_W - 1:CONV_W, cols] * xl_b
            for k in range(1, CONV_W):
                xc_b = xc_b + cw_ref[CONV_W - 1 - k:CONV_W - k, cols] * xl_s[pl.ds(base + SUBLANES - k, t_blk), cols]
            xc_parts.append(xc_b)
        xc = jnp.concatenate(xc_parts, axis=0)
        filler()
        pre = _dot(xc.astype(BF16), wg_ref[g])
        r = _sigmoid(pre[:, :GATE_GROUP] + b_rg_ref[:, cols])
        filler()
        ig = _sigmoid(pre[:, GATE_GROUP:] + b_ig_ref[:, cols])
        log_a = c_lru[:, cols] * r
        to_slabs(a_s, jnp.exp(log_a), g * (GATE_GROUP // LANES))
        filler()
        u = _sqrt_nonneg(_neg_expm1_2x(log_a)) * (ig * xc)
        to_slabs(u_s, u, g * (GATE_GROUP // LANES))
        filler()
    filler(len(side))

    h0 = tuple(h_ref[:, j * LANES:(j + 1) * LANES] for j in range(N_SLAB_W))

    def chunk_body(c, h):
        def step(t, st):
            h, bc = st
            tt = c * GLA_CHUNK + t
            h_new, bc_new = [], []
            for j in range(N_SLAB_W):
                idx = (j, pl.ds(tt, nb, stride=pitch), slice(None))
                hj = a_s[idx] * h[j] + u_s[idx]
                u_s[idx] = hj
                h_new.append(hj)
            for j in range(N_SLAB_K):
                idx = (j, pl.ds(tt, nb, stride=pitch), slice(None))
                bj = bc[j] + g_s[idx]
                g_s[idx] = bj
                bc_new.append(bj)
            return tuple(h_new), tuple(bc_new)

        bc0 = tuple(jnp.zeros((nb, LANES), F32) for _ in range(N_SLAB_K))
        h, _ = lax.fori_loop(0, GLA_CHUNK, step, (h, bc0), unroll=4)
        return h

    h_fin = lax.fori_loop(0, n_chunks, chunk_body, h0)
    for j in range(N_SLAB_W):
        h_ref[:, j * LANES:(j + 1) * LANES] = h_fin[j]

    causal = _causal_mask(GLA_CHUNK)
    g_head = g_head_ref[...]
    for b in range(nb):
        blk = slice(b * t_blk, (b + 1) * t_blk)
        h_b = jnp.concatenate([u_s[j, b * pitch:b * pitch + t_blk, :] for j in range(N_SLAB_W)], axis=1)
        ycat_s[blk, 0:W_LRU] = (h_b * sz_s[blk, :]).astype(BF16)
    for c in range(n_chunks):
        items = []
        for b in range(nb):
            rows = slice(b * t_blk + c * GLA_CHUNK, b * t_blk + (c + 1) * GLA_CHUNK)
            srow = b * pitch + c * GLA_CHUNK
            bc = jnp.concatenate([g_s[j, srow:srow + GLA_CHUNK, :] for j in range(N_SLAB_K)], axis=1)
            bl = bc[GLA_CHUNK - 1:GLA_CHUNK, :]
            k_c = k_s[rows, :]
            q_t = (q_s[rows, :] * jnp.exp(bc)).astype(BF16)
            k_t = (k_c * jnp.exp(-bc)).astype(BF16)
            k_end = k_c * jnp.exp(bl - bc)
            dec = jnp.exp(bl)
            for hd in range(GLA_HEADS):
                ks = slice(hd * HEAD_K, (hd + 1) * HEAD_K)
                vs = slice(hd * HEAD_V, (hd + 1) * HEAD_V)
                items.append((q_t[:, ks], k_t[:, ks], k_end[:, ks], v_s[rows, vs], dec[:, ks], s_ref[b, hd]))
        results = _gla_chunk_heads(items, causal)
        for b in range(nb):
            rows = slice(b * t_blk + c * GLA_CHUNK, b * t_blk + (c + 1) * GLA_CHUNK)
            o_heads = []
            for hd in range(GLA_HEADS):
                o, s_new = results[b * GLA_HEADS + hd]
                s_ref[b, hd] = s_new
                o_heads.append(_rms(o, g_head))
            o_c = jnp.concatenate(o_heads, axis=1)
            ycat_s[rows, W_LRU:W_LRU + GLA_DV] = (o_c * szg_s[rows, :]).astype(BF16)
    y = _dot(ycat_s[...], w_out_ref[...])
    out = x_ref[...].reshape(rows_all, D_MODEL) + _rms(y, g_post_ref[...])
    y_ref[...] = out.reshape(nb, t_blk, D_MODEL)

    @pl.when(i == pl.num_programs(0) - 1)
    def _():
        for b in range(nb):
            last = xl_s[b * pitch + t_blk + hist:b * pitch + t_blk + SUBLANES, :]
            for m in range(CONV_W - 1):
                conv_ref[m, b:b + 1, :] = last[m:m + 1, :]


def _spread_rows(slab, by_step, n_seq, t_len, width):
    n_slab = width // LANES
    slab[0:n_slab] = jnp.zeros((n_slab,) + slab.shape[1:], F32)
    for t, val in by_step.items():
        for j in range(n_slab):
            slab[j, pl.ds(t, n_seq, stride=t_len), :] = val[:, j * LANES:(j + 1) * LANES]
    return jnp.concatenate([slab[j] for j in range(n_slab)], axis=1)


def _gather_step(slab, arr, t, n_seq, t_len):
    n_slab = arr.shape[1] // LANES
    for j in range(n_slab):
        slab[j] = arr[:, j * LANES:(j + 1) * LANES]
    return jnp.concatenate([slab[j, pl.ds(t, n_seq, stride=t_len), :] for j in range(n_slab)], axis=1)


def _sample_pre_kernel(x_ref, h0_ref, conv0_ref, g_pre_ref, w_in_ref, cw_ref, cb_ref, wg_ref, b_rg_ref, b_ig_ref,
                       lam_ref, w_gk2_ref, b_gk_ref,
                       qt_ref, kt_ref, ke_ref, v_ref, dec_ref, ylru_ref, szg_ref, hnew_ref, convnew_ref,
                       slab_a, slab_b, *, n_seq, t_len):
    rows = n_seq * t_len
    hist = t_len - (CONV_W - 1)

    def steps(v):
        return v.reshape(n_seq, t_len, v.shape[-1])

    def t_index(width):
        return lax.broadcasted_iota(jnp.int32, (n_seq, t_len, width), 1)

    x = x_ref[...]
    xn = _rms(x, g_pre_ref[...]).astype(BF16)

    xl = _dot(xn, w_in_ref[:, OFF_XL:OFF_XL + W_LRU])
    past = _spread_rows(slab_a, {hist + m: conv0_ref[m] for m in range(CONV_W - 1)}, n_seq, t_len, W_LRU)
    xl3, past3 = steps(xl), steps(past)
    t_w = t_index(W_LRU)
    xc3 = cb_ref[...][None] + cw_ref[CONV_W - 1:CONV_W, :][None] * xl3
    for k in range(1, CONV_W):
        src = jnp.where(t_w <= t_len - 1 - k, xl3, past3)
        xc3 = xc3 + cw_ref[CONV_W - 1 - k:CONV_W - k, :][None] * pltpu.roll(src, k, axis=1)
    xc = xc3.reshape(rows, W_LRU)
    xcb = xc.astype(BF16)
    r_parts, i_parts = [], []
    for g in range(N_GATE_GROUPS):
        pre = _dot(xcb[:, g * GATE_GROUP:(g + 1) * GATE_GROUP], wg_ref[g])
        r_parts.append(pre[:, :GATE_GROUP])
        i_parts.append(pre[:, GATE_GROUP:])
    r = _sigmoid(jnp.concatenate(r_parts, axis=1) + b_rg_ref[...])
    ig = _sigmoid(jnp.concatenate(i_parts, axis=1) + b_ig_ref[...])
    log_a = (LRU_C * _log_sigmoid(lam_ref[...])) * r
    a_cum = steps(jnp.exp(log_a))
    h_acc = steps(_sqrt_nonneg(_neg_expm1_2x(log_a)) * (ig * xc))
    shift = 1
    while shift < t_len:
        m = t_w >= shift
        h_prev = jnp.where(m, pltpu.roll(h_acc, shift, axis=1), 0.0)
        a_prev = jnp.where(m, pltpu.roll(a_cum, shift, axis=1), 1.0)
        h_acc = a_cum * h_prev + h_acc
        a_cum = a_cum * a_prev
        shift *= 2
    h0_all = steps(_spread_rows(slab_b, {t: h0_ref[...] for t in range(t_len)}, n_seq, t_len, W_LRU))
    h = (h_acc + a_cum * h0_all).reshape(rows, W_LRU)
    hnew_ref[...] = _gather_step(slab_a, h, t_len - 1, n_seq, t_len)
    for j in range(N_SLAB_W):
        slab_b[j] = xl[:, j * LANES:(j + 1) * LANES]
    for m in range(CONV_W - 1):
        for j in range(N_SLAB_W):
            convnew_ref[m, :, j * LANES:(j + 1) * LANES] = slab_b[j, pl.ds(hist + m, n_seq, stride=t_len), :]
    z = _dot(xn, w_in_ref[:, OFF_ZL:OFF_ZL + W_LRU])
    ylru_ref[...] = (h * (z * _sigmoid(z))).astype(BF16)

    q = _dot(xn, w_in_ref[:, OFF_Q:OFF_Q + GLA_DK]) * (HEAD_K ** -0.5)
    k = _dot(xn, w_in_ref[:, OFF_K:OFF_K + GLA_DK])
    v_ref[...] = _dot(xn, w_in_ref[:, OFF_V:OFF_V + GLA_DV])
    zg = _dot(xn, w_in_ref[:, OFF_ZG:OFF_ZG + GLA_DV])
    szg_ref[...] = zg * _sigmoid(zg)
    rg = _dot(xn, w_in_ref[:, OFF_RG:D_IN]).astype(BF16)
    gl = _log_sigmoid(_dot(rg, w_gk2_ref[...]) + b_gk_ref[...]) * (1.0 / GK_TAU)
    t_k = t_index(GLA_DK)
    bc = steps(gl)
    tot = bc
    shift = 1
    while shift < t_len:
        bc = bc + jnp.where(t_k >= shift, pltpu.roll(bc, shift, axis=1), 0.0)
        tot = tot + pltpu.roll(tot, shift, axis=1)
        shift *= 2
    bc = bc.reshape(rows, GLA_DK)
    tot = tot.reshape(rows, GLA_DK)
    qt_ref[...] = q * jnp.exp(bc)
    kt_ref[...] = k * jnp.exp(-bc)
    ke_ref[...] = k * jnp.exp(tot - bc)
    dec_ref[...] = _gather_step(slab_a, jnp.exp(tot), 0, n_seq, t_len)


def _sample_state_kernel(qt_ref, kt_ref, ke_ref, v_ref, dec_ref, s0_ref, o_ref, s_ref, *, n_seq, t_len):
    causal = _causal_mask(t_len)
    items = []
    for b in range(n_seq):
        rows = slice(b * t_len, (b + 1) * t_len)
        for hd in range(GLA_HEADS):
            ks = slice(hd * HEAD_K, (hd + 1) * HEAD_K)
            vs = slice(hd * HEAD_V, (hd + 1) * HEAD_V)
            items.append((qt_ref[rows, ks].astype(BF16), kt_ref[rows, ks].astype(BF16), ke_ref[rows, ks],
                          v_ref[rows, vs].astype(BF16), dec_ref[b:b + 1, ks], s0_ref[b, hd]))
    results = _gla_chunk_heads(items, causal)
    for b in range(n_seq):
        rows = slice(b * t_len, (b + 1) * t_len)
        for hd in range(GLA_HEADS):
            o, s_new = results[b * GLA_HEADS + hd]
            s_ref[b, hd] = s_new
            o_ref[rows, hd * HEAD_V:(hd + 1) * HEAD_V] = o


def _sample_post_kernel(x_ref, o_ref, szg_ref, ylru_ref, g_head_ref, w_out_ref, g_post_ref, y_ref):
    g_head = g_head_ref[...]
    o_n = jnp.concatenate(
        [_rms(o_ref[:, hd * HEAD_V:(hd + 1) * HEAD_V], g_head) for hd in range(GLA_HEADS)], axis=1)
    y_gla = (o_n * szg_ref[...]).astype(BF16)
    y = _dot(jnp.concatenate([ylru_ref[...], y_gla], axis=1), w_out_ref[...])
    y_ref[...] = x_ref[...] + _rms(y, g_post_ref[...])


def _sample_call(x, h0, conv0, s0, params, seq_blk, state_blk):
    (g_pre, w_in, cw, cb, wg, b_rg, b_ig, lam, w_gk2, b_gk, g_head, w_out, g_post) = params
    n_seq, t_len, _ = x.shape
    assert t_len == SUBLANES and t_len >= CONV_W - 1
    rows = n_seq * t_len
    x2 = x.reshape(rows, D_MODEL)
    conv0 = jnp.transpose(conv0, (1, 0, 2))
    rb = seq_blk * t_len

    def row_spec(width, blk_rows):
        return pl.BlockSpec((blk_rows, width), lambda i: (i, 0))

    conv_spec = pl.BlockSpec((CONV_W - 1, seq_blk, W_LRU), lambda i: (0, i, 0))

    cparams = pltpu.CompilerParams(dimension_semantics=("arbitrary",), vmem_limit_bytes=56 * 1024 * 1024)
    pre_params = (g_pre, w_in, cw, cb, wg, b_rg, b_ig, lam, w_gk2, b_gk)
    qt, kt, ke, v, dec, ylru, szg, h_new, conv_new = pl.pallas_call(
        functools.partial(_sample_pre_kernel, n_seq=seq_blk, t_len=t_len),
        grid=(n_seq // seq_blk,),
        in_specs=[row_spec(D_MODEL, rb), row_spec(W_LRU, seq_blk), conv_spec]
        + [_vmem_full()] * len(pre_params),
        out_specs=(row_spec(GLA_DK, rb), row_spec(GLA_DK, rb), row_spec(GLA_DK, rb), row_spec(GLA_DV, rb),
                   row_spec(GLA_DK, seq_blk), row_spec(W_LRU, rb), row_spec(GLA_DV, rb),
                   row_spec(W_LRU, seq_blk), conv_spec),
        out_shape=(jax.ShapeDtypeStruct((rows, GLA_DK), F32), jax.ShapeDtypeStruct((rows, GLA_DK), F32),
                   jax.ShapeDtypeStruct((rows, GLA_DK), F32), jax.ShapeDtypeStruct((rows, GLA_DV), F32),
                   jax.ShapeDtypeStruct((n_seq, GLA_DK), F32), jax.ShapeDtypeStruct((rows, W_LRU), BF16),
                   jax.ShapeDtypeStruct((rows, GLA_DV), F32), jax.ShapeDtypeStruct((n_seq, W_LRU), F32),
                   jax.ShapeDtypeStruct((CONV_W - 1, n_seq, W_LRU), F32)),
        scratch_shapes=[pltpu.VMEM((N_SLAB_W, rb, LANES), F32), pltpu.VMEM((N_SLAB_W, rb, LANES), F32)],
        compiler_params=cparams,
        name="sample_pre",
    )(x2, h0, conv0, *pre_params)

    sb = state_blk * t_len
    s_spec = pl.BlockSpec((state_blk, GLA_HEADS, HEAD_K, HEAD_V), lambda i: (i, 0, 0, 0))
    o, s_new = pl.pallas_call(
        functools.partial(_sample_state_kernel, n_seq=state_blk, t_len=t_len),
        grid=(n_seq // state_blk,),
        in_specs=[row_spec(GLA_DK, sb), row_spec(GLA_DK, sb), row_spec(GLA_DK, sb), row_spec(GLA_DV, sb),
                  row_spec(GLA_DK, state_blk), s_spec],
        out_specs=(row_spec(GLA_DV, sb), s_spec),
        out_shape=(jax.ShapeDtypeStruct((rows, GLA_DV), F32),
                   jax.ShapeDtypeStruct((n_seq, GLA_HEADS, HEAD_K, HEAD_V), F32)),
        compiler_params=cparams,
        name="sample_state",
    )(qt, kt, ke, v, dec, s0)

    y = pl.pallas_call(
        _sample_post_kernel,
        grid=(n_seq // seq_blk,),
        in_specs=[row_spec(D_MODEL, rb), row_spec(GLA_DV, rb), row_spec(GLA_DV, rb), row_spec(W_LRU, rb),
                  _vmem_full(), _vmem_full(), _vmem_full()],
        out_specs=row_spec(D_MODEL, rb),
        out_shape=jax.ShapeDtypeStruct((rows, D_MODEL), F32),
        compiler_params=cparams,
        name="sample_post",
    )(x2, o, szg, ylru, g_head, w_out, g_post)
    return (y.reshape(n_seq, t_len, D_MODEL), h_new, jnp.transpose(conv_new, (1, 0, 2)), s_new)


def _gate_weights(w_rg, w_ig):
    per = GATE_GROUP // LRU_BLOCK

    def bd(w):
        w = w.reshape(N_GATE_GROUPS, per, LRU_BLOCK, LRU_BLOCK)
        eye = jnp.eye(per, dtype=w.dtype)
        w = w[:, :, :, None, :] * eye[None, :, None, :, None]
        return w.reshape(N_GATE_GROUPS, GATE_GROUP, GATE_GROUP)

    return jnp.concatenate([bd(w_rg), bd(w_ig)], axis=2).astype(BF16)


def _vmem_full():
    return pl.BlockSpec(memory_space=pltpu.VMEM)


def _prompt_call(x, params, t_blk):
    nb, seq, _ = x.shape
    pitch = t_blk + SUBLANES
    n_t = seq // t_blk
    kern = functools.partial(_prompt_kernel, nb=nb, t_blk=t_blk, pitch=pitch)
    x_spec = pl.BlockSpec((nb, t_blk, D_MODEL), lambda i: (0, i, 0))
    out_shape = (
        jax.ShapeDtypeStruct((nb, seq, D_MODEL), F32),
        jax.ShapeDtypeStruct((nb, W_LRU), F32),
        jax.ShapeDtypeStruct((CONV_W - 1, nb, W_LRU), F32),
        jax.ShapeDtypeStruct((nb, GLA_HEADS, HEAD_K, HEAD_V), F32),
    )
    out_specs = (
        x_spec,
        pl.BlockSpec((nb, W_LRU), lambda i: (0, 0)),
        pl.BlockSpec((CONV_W - 1, nb, W_LRU), lambda i: (0, 0, 0)),
        pl.BlockSpec((nb, GLA_HEADS, HEAD_K, HEAD_V), lambda i: (0, 0, 0, 0)),
    )
    rows = nb * t_blk
    scratch = [
        pltpu.VMEM((N_SLAB_W, nb * pitch, LANES), F32),
        pltpu.VMEM((N_SLAB_W, nb * pitch, LANES), F32),
        pltpu.VMEM((N_SLAB_K, nb * pitch, LANES), F32),
        pltpu.VMEM((nb * pitch, W_LRU), F32),
        pltpu.VMEM((rows, W_LRU), F32),
        pltpu.VMEM((rows, GLA_DK), F32),
        pltpu.VMEM((rows, GLA_DK), F32),
        pltpu.VMEM((rows, GLA_DV), BF16),
        pltpu.VMEM((rows, GLA_DV), F32),
        pltpu.VMEM((rows, W_LRU + GLA_DV), BF16),
    ]
    return pl.pallas_call(
        kern,
        grid=(n_t,),
        in_specs=[x_spec] + [_vmem_full()] * len(params),
        out_specs=out_specs,
        out_shape=out_shape,
        scratch_shapes=scratch,
        compiler_params=pltpu.CompilerParams(
            dimension_semantics=("arbitrary",),
            vmem_limit_bytes=56 * 1024 * 1024,
        ),
        name="prompt_mixer",
    )(x, *params)


def kernel(x_prompt, x_sample, state_lru_h, state_lru_conv, state_gla, g_pre, w_in, conv_w, conv_b, w_rg, b_rg,
           w_ig, b_ig, lru_lambda, w_gk2, b_gk, g_head, w_out, g_post):
    row = lambda p: p[0].reshape(1, -1)
    params = (
        row(g_pre), w_in[0].astype(BF16), conv_w[0], row(conv_b), _gate_weights(w_rg[0], w_ig[0]),
        row(b_rg), row(b_ig), row(lru_lambda), w_gk2[0].astype(BF16), row(b_gk), row(g_head),
        w_out[0].astype(BF16), row(g_post),
    )
    yp, hp, cp, sp = _prompt_call(x_prompt, params, t_blk=64)
    ys, hs, cs, ss = _sample_call(x_sample, state_lru_h[0], state_lru_conv[0], state_gla[0], params,
                                  seq_blk=64, state_blk=8)
    cp = jnp.transpose(cp, (1, 0, 2))
    return yp, ys, hp[None], cp[None], sp[None], hs[None], cs[None], ss[None]
```
